```python
import math
import jax, jax.numpy as jnp
from jax import lax
import numpy as np

D_MODEL = 1024
BATCH = 4
SEQ = 4096
DEPTH = 1

SB_HEADS = 8
SB_HEAD_DIM = 64
SB_WIDTH = SB_HEADS * SB_HEAD_DIM
MLA_HEADS = 8
MLA_NOPE_DIM = 64
MLA_ROPE_DIM = 32
MLA_V_DIM = 64
MLA_Q_RANK = 384
MLA_KV_RANK = 256
MLA_WIDTH = MLA_HEADS * MLA_V_DIM
MLA_QK_DIM = MLA_NOPE_DIM + MLA_ROPE_DIM

Q_BLOCK = 128
ROPE_BASE = 10000.0
EPS = 1e-6

SPLITS = (SB_WIDTH, SB_WIDTH, SB_WIDTH, SB_WIDTH,
          MLA_Q_RANK, MLA_KV_RANK, MLA_ROPE_DIM, MLA_WIDTH,
          D_MODEL, D_MODEL)
IN_WIDTH = int(sum(SPLITS))
SPLIT_POINTS = tuple(int(v) for v in np.cumsum(SPLITS)[:-1])

kernel_name = "hybrid_stickbreaking_mla_adaln_block"


def rmsnorm(x, g):
    xf = x.astype(jnp.float32)
    y = xf * lax.rsqrt(jnp.mean(xf * xf, axis=-1, keepdims=True) + EPS)
    return (y * g.astype(jnp.float32)).astype(x.dtype)


def rope(x, positions):
    r = x.shape[-1]
    inv_freq = ROPE_BASE ** (-jnp.arange(0, r, 2, dtype=jnp.float32) / r)
    ang = positions.astype(jnp.float32)[:, :, None, None] * inv_freq
    cos, sin = jnp.cos(ang), jnp.sin(ang)
    xf = x.astype(jnp.float32)
    x1, x2 = xf[..., : r // 2], xf[..., r // 2:]
    return jnp.concatenate([x1 * cos - x2 * sin, x1 * sin + x2 * cos], axis=-1)


def to_blocks(t):
    b, h, s, d = t.shape
    return t.reshape(b, h, s // Q_BLOCK, Q_BLOCK, d).transpose(2, 0, 1, 3, 4)


def from_blocks(t):
    n, b, h, q, d = t.shape
    return t.transpose(1, 2, 0, 3, 4).reshape(b, h, n * q, d)


def stick_breaking_attention(q, k, v):
    s_len = k.shape[2]
    scale = 1.0 / math.sqrt(q.shape[-1])
    key_idx = jnp.arange(s_len)

    def block(args):
        qb, i = args
        z = jnp.einsum('bhqd,bhkd->bhqk', qb, k) * scale
        q_idx = i * Q_BLOCK + jnp.arange(Q_BLOCK)
        strict = key_idx[None, :] < q_idx[:, None]
        log_one_minus = jnp.where(strict, jax.nn.log_sigmoid(-z), 0.0)
        suffix = lax.cumsum(log_one_minus, axis=3, reverse=True) - log_one_minus
        w = jnp.where(strict, jnp.exp(jax.nn.log_sigmoid(z) + suffix), 0.0)
        return jnp.einsum('bhqk,bhkd->bhqd', w, v)

    n_blk = s_len // Q_BLOCK
    out = lax.map(block, (to_blocks(q), jnp.arange(n_blk)))
    return from_blocks(out)


def latent_attention(q_nope, q_pe, k_nope, k_pe, v):
    s_len = k_nope.shape[2]
    scale = 1.0 / math.sqrt(MLA_QK_DIM)
    key_idx = jnp.arange(s_len)

    def block(args):
        qn, qp, i = args
        sc = (jnp.einsum('bhqd,bhkd->bhqk', qn, k_nope)
              + jnp.einsum('bhqr,bkr->bhqk', qp, k_pe)) * scale
        q_idx = i * Q_BLOCK + jnp.arange(Q_BLOCK)
        causal = key_idx[None, :] <= q_idx[:, None]
        p = jax.nn.softmax(jnp.where(causal, sc, -jnp.inf), axis=-1)
        return jnp.einsum('bhqk,bhkd->bhqd', p, v)

    n_blk = s_len // Q_BLOCK
    out = lax.map(block, (to_blocks(q_nope), to_blocks(q_pe), jnp.arange(n_blk)))
    return from_blocks(out)


def split_heads(t, h):
    b, s, w = t.shape
    return t.reshape(b, s, h, w // h).transpose(0, 2, 1, 3)


def merge_heads(t):
    b, h, s, d = t.shape
    return t.transpose(0, 2, 1, 3).reshape(b, s, h * d)


def setup_inputs(seed: int = 0) -> dict:
    key = jax.random.key(seed)
    ks = jax.random.split(key, 16)
    f32 = jnp.float32

    def nrm(k, shape, fan_in, mult=1.0):
        return jax.random.normal(k, shape, f32) * (mult * fan_in ** -0.5)

    x = jax.random.normal(ks[0], (BATCH, SEQ, D_MODEL), f32)
    c = jax.random.normal(ks[1], (BATCH, D_MODEL), f32)
    positions = jnp.broadcast_to(jnp.arange(SEQ, dtype=jnp.int32), (BATCH, SEQ))
    w_ada = nrm(ks[2], (DEPTH, D_MODEL, 3 * D_MODEL), D_MODEL, 0.2)
    b_ada = jax.random.normal(ks[3], (DEPTH, 3 * D_MODEL), f32) * 0.02
    norm_gain = 1.0 + 0.02 * jax.random.normal(ks[4], (DEPTH, D_MODEL), f32)
    w_in = nrm(ks[5], (DEPTH, D_MODEL, IN_WIDTH), D_MODEL)
    q_norm_gain = 1.0 + 0.02 * jax.random.normal(ks[6], (DEPTH, MLA_Q_RANK), f32)
    w_uq = nrm(ks[7], (DEPTH, MLA_Q_RANK, MLA_HEADS * MLA_QK_DIM), MLA_Q_RANK)
    kv_norm_gain = 1.0 + 0.02 * jax.random.normal(ks[8], (DEPTH, MLA_KV_RANK), f32)
    w_ukv = nrm(ks[9], (DEPTH, MLA_KV_RANK, MLA_HEADS * (MLA_NOPE_DIM + MLA_V_DIM)), MLA_KV_RANK)
    w_branch_a = nrm(ks[10], (DEPTH, SB_WIDTH, D_MODEL), SB_WIDTH)
    w_branch_b = nrm(ks[11], (DEPTH, MLA_WIDTH, D_MODEL), MLA_WIDTH)
    w_out = nrm(ks[12], (DEPTH, D_MODEL, D_MODEL), D_MODEL)
    final_norm_gain = 1.0 + 0.02 * jax.random.normal(ks[13], (D_MODEL,), f32)
    return {"x": x, "c": c, "positions": positions, "w_ada": w_ada, "b_ada": b_ada,
            "norm_gain": norm_gain, "w_in": w_in, "q_norm_gain": q_norm_gain, "w_uq": w_uq,
            "kv_norm_gain": kv_norm_gain, "w_ukv": w_ukv, "w_branch_a": w_branch_a,
            "w_branch_b": w_branch_b, "w_out": w_out, "final_norm_gain": final_norm_gain}


def reference(x, c, positions, w_ada, b_ada, norm_gain, w_in, q_norm_gain, w_uq,
              kv_norm_gain, w_ukv, w_branch_a, w_branch_b, w_out, final_norm_gain):
    b, s, _ = x.shape
    f32 = jnp.float32
    for l in range(DEPTH):
        mod = c @ w_ada[l] + b_ada[l]
        shift, scale, gate = jnp.split(mod, 3, axis=-1)
        h = rmsnorm(x, norm_gain[l]) * (1.0 + scale[:, None, :]) + shift[:, None, :]

        proj = h @ w_in[l]
        (sb_q, sb_k, sb_v, sb_z, c_q, c_kv, k_rot, mla_z, g_a, g_b) = jnp.split(proj, SPLIT_POINTS, axis=-1)

        o_a = stick_breaking_attention(split_heads(sb_q, SB_HEADS).astype(f32),
                                       split_heads(sb_k, SB_HEADS).astype(f32),
                                       split_heads(sb_v, SB_HEADS).astype(f32))
        o_a = merge_heads(o_a).astype(x.dtype)
        y_a = (o_a * jax.nn.silu(sb_z)) @ w_branch_a[l]

        q = (rmsnorm(c_q, q_norm_gain[l]) @ w_uq[l]).reshape(b, s, MLA_HEADS, MLA_QK_DIM)
        q_nope = q[..., :MLA_NOPE_DIM].astype(f32)
        q_pe = rope(q[..., MLA_NOPE_DIM:], positions)
        kv = (rmsnorm(c_kv, kv_norm_gain[l]) @ w_ukv[l]).reshape(b, s, MLA_HEADS, MLA_NOPE_DIM + MLA_V_DIM)
        k_nope = kv[..., :MLA_NOPE_DIM].astype(f32)
        v_b = kv[..., MLA_NOPE_DIM:].astype(f32)
        k_pe = rope(k_rot[:, :, None, :], positions)[:, :, 0, :]
        o_b = latent_attention(q_nope.transpose(0, 2, 1, 3), q_pe.transpose(0, 2, 1, 3),
                               k_nope.transpose(0, 2, 1, 3), k_pe, v_b.transpose(0, 2, 1, 3))
        o_b = merge_heads(o_b).astype(x.dtype)
        y_b = (o_b * jax.nn.silu(mla_z)) @ w_branch_b[l]

        merged = jax.nn.sigmoid(g_a) * y_a + jax.nn.sigmoid(g_b) * y_b
        x = x + gate[:, None, :] * (merged @ w_out[l])
    return rmsnorm(x, final_norm_gain)
```

```python
import functools
import math

import numpy as np
import jax
import jax.numpy as jnp
from jax import lax
from jax.experimental import pallas as pl
from jax.experimental.pallas import tpu as pltpu

F32 = jnp.float32
BF16 = jnp.bfloat16

D_MODEL = 1024
SB_HEADS = 8
SB_HEAD_DIM = 64
SB_WIDTH = SB_HEADS * SB_HEAD_DIM
MLA_HEADS = 8
MLA_NOPE_DIM = 64
MLA_ROPE_DIM = 32
MLA_V_DIM = 64
MLA_Q_RANK = 384
MLA_KV_RANK = 256
MLA_WIDTH = MLA_HEADS * MLA_V_DIM
MLA_QK_DIM = MLA_NOPE_DIM + MLA_ROPE_DIM
ROPE_BASE = 10000.0
EPS = 1e-6

LANES = 128
HEAD_PAIR = 2
MLA_PAD_WIDTH = MLA_HEADS * LANES

ROW_TILE = 512
ATTN_BLOCK = 256
VMEM_LIMIT = 56 * 1024 * 1024

NEG_BIG = -1e30

_C_SBQ = 0
_C_SBK = _C_SBQ + SB_WIDTH
_C_SBV = _C_SBK + SB_WIDTH
_C_CQ = _C_SBV + SB_WIDTH
_C_CKV = _C_CQ + MLA_Q_RANK
_C_KP = _C_CKV + MLA_KV_RANK
_C_KR = _C_KP + LANES
_C_END = _C_KR + LANES

_NT = (((1,), (1,)), ((), ()))


def _dot(a, b):
    return jnp.dot(a, b, preferred_element_type=F32)


def _split_bf16(v):
    hi = v.astype(BF16)
    lo = (v - hi.astype(F32)).astype(BF16)
    return hi, lo


def _sigmoid(v):
    return 1.0 / (1.0 + jnp.exp(-v))


def _rms_scale(v):
    return v * lax.rsqrt(jnp.mean(v * v, axis=-1, keepdims=True) + EPS)


def _adaln(x, mod_ref, ng_ref):
    shift = mod_ref[0, 0:1, :]
    scale = mod_ref[0, 1:2, :]
    return _rms_scale(x) * ng_ref[...] * (1.0 + scale) + shift


def _mod_kernel(c_ref, w_ref, b_ref, o_ref):
    c_hi, c_lo = _split_bf16(c_ref[...])
    w_hi, w_lo = _split_bf16(w_ref[...])
    o_ref[...] = _dot(c_hi, w_hi) + _dot(c_hi, w_lo) + _dot(c_lo, w_hi) + b_ref[...]


def _mod_call(c_pad, w_ada, b_ada):
    rows = c_pad.shape[0]
    n = w_ada.shape[1]
    tn = D_MODEL
    return pl.pallas_call(
        _mod_kernel,
        grid=(n // tn,),
        in_specs=[pl.BlockSpec((rows, D_MODEL), lambda j: (0, 0)),
                  pl.BlockSpec((D_MODEL, tn), lambda j: (0, j)),
                  pl.BlockSpec((1, tn), lambda j: (0, j))],
        out_specs=pl.BlockSpec((rows, tn), lambda j: (0, j)),
        out_shape=jax.ShapeDtypeStruct((rows, n), F32),
        compiler_params=pltpu.CompilerParams(dimension_semantics=("arbitrary",),
                                             vmem_limit_bytes=VMEM_LIMIT),
        name="adaln_mod",
    )(c_pad, w_ada, b_ada)


def _proj_kernel(x_ref, mod_ref, ng_ref, pos_ref, invf_ref, w1_ref, qg_ref, wuq_ref, kvg_ref, wukv_ref,
                 qsb_ref, ksb_ref, vsb_ref, qf_ref, kf_ref, vm_ref):
    hb = _adaln(x_ref[0], mod_ref, ng_ref).astype(BF16)

    qsb_ref[0] = _dot(hb, w1_ref[:, _C_SBQ:_C_SBK]).astype(BF16)
    ksb_ref[0] = _dot(hb, w1_ref[:, _C_SBK:_C_SBV]).astype(BF16)
    vsb_ref[0] = _dot(hb, w1_ref[:, _C_SBV:_C_CQ]).astype(BF16)

    ang = pos_ref[0].astype(F32) * invf_ref[...]
    cosv = jnp.cos(ang)
    sinv = jnp.sin(ang)

    cqn = (_rms_scale(_dot(hb, w1_ref[:, _C_CQ:_C_CKV])) * qg_ref[...]).astype(BF16)
    q_scale = 1.0 / math.sqrt(MLA_QK_DIM)
    cos_q = cosv * q_scale
    sin_q = sinv * q_scale
    for hh in range(MLA_HEADS):
        lo = hh * LANES
        q_pad = _dot(cqn, wuq_ref[:, lo:lo + LANES])
        q_rot = _dot(cqn, wuq_ref[:, MLA_PAD_WIDTH + lo:MLA_PAD_WIDTH + lo + LANES])
        qf_ref[0, :, lo:lo + LANES] = (q_pad * cos_q + q_rot * sin_q).astype(BF16)

    ckvn = (_rms_scale(_dot(hb, w1_ref[:, _C_CKV:_C_KP])) * kvg_ref[...]).astype(BF16)
    k_pe = _dot(hb, w1_ref[:, _C_KP:_C_KR]) * cosv + _dot(hb, w1_ref[:, _C_KR:_C_END]) * sinv
    for hh in range(MLA_HEADS):
        lo = hh * LANES
        kf_ref[0, :, lo:lo + LANES] = (_dot(ckvn, wukv_ref[:, lo:lo + LANES]) + k_pe).astype(BF16)
    vm_ref[0] = _dot(ckvn, wukv_ref[:, MLA_PAD_WIDTH:MLA_PAD_WIDTH + MLA_WIDTH]).astype(BF16)


def _proj_call(x, mod3, ng, pos3, invf, w1, qg, wuq, kvg, wukv):
    b, s, _ = x.shape
    tm = ROW_TILE
    row = lambda width: pl.BlockSpec((1, tm, width), lambda i, j: (i, j, 0))
    full = lambda a: pl.BlockSpec(a.shape, lambda i, j: (0,) * a.ndim)
    out = lambda width: jax.ShapeDtypeStruct((b, s, width), BF16)
    return pl.pallas_call(
        _proj_kernel,
        grid=(b, s // tm),
        in_specs=[row(D_MODEL),
                  pl.BlockSpec((1, 3, D_MODEL), lambda i, j: (i, 0, 0)),
                  full(ng), row(1), full(invf), full(w1), full(qg), full(wuq), full(kvg), full(wukv)],
        out_specs=[row(SB_WIDTH), row(SB_WIDTH), row(SB_WIDTH),
                   row(MLA_PAD_WIDTH), row(MLA_PAD_WIDTH), row(MLA_WIDTH)],
        out_shape=[out(SB_WIDTH), out(SB_WIDTH), out(SB_WIDTH),
                   out(MLA_PAD_WIDTH), out(MLA_PAD_WIDTH), out(MLA_WIDTH)],
        compiler_params=pltpu.CompilerParams(dimension_semantics=("arbitrary", "arbitrary"),
                                             vmem_limit_bytes=VMEM_LIMIT),
        name="attn_operand_proj",
    )(x, mod3, ng, pos3, invf, w1, qg, wuq, kvg, wukv)


def _sb_kernel(q_ref, k_ref, v_ref, tri_ref, o_ref):
    bq = bk = ATTN_BLOCK
    qi = pl.program_id(2)
    q = q_ref[0]
    lane = lax.broadcasted_iota(jnp.int32, (bq, LANES), 1)
    strict = (lax.broadcasted_iota(jnp.int32, (bq, bk), 1)
              < lax.broadcasted_iota(jnp.int32, (bq, bk), 0))
    tri = tri_ref[...]

    def block(qh, kb, carry, acc, masked):
        start = pl.multiple_of(kb * bk, bk)
        k = k_ref[0, pl.ds(start, bk), :]
        v = v_ref[0, pl.ds(start, bk), :]
        z = lax.dot_general(qh, k, _NT, preferred_element_type=F32)
        sp = jnp.maximum(z, 0.0) + jnp.log(1.0 + jnp.exp(-jnp.abs(z)))
        if masked:
            sp = jnp.where(strict, sp, 0.0)
        sp_hi, sp_lo = _split_bf16(sp)
        incl = _dot(sp_hi, tri) + _dot(sp_lo, tri)
        w = jnp.exp(z - incl - carry)
        if masked:
            w = jnp.where(strict, w, 0.0)
        acc = acc + _dot(w.astype(BF16), v)
        carry = carry + incl[:, 0:1]
        return carry, acc

    outs = []
    for hh in range(HEAD_PAIR):
        own = (lane >= hh * SB_HEAD_DIM) & (lane < (hh + 1) * SB_HEAD_DIM)
        qh = jnp.where(own, q, jnp.zeros_like(q))
        carry, acc = block(qh, qi, jnp.zeros((bq, 1), F32), jnp.zeros((bq, LANES), F32), True)
        carry, acc = lax.fori_loop(
            0, qi, lambda i, c, qh=qh: block(qh, qi - 1 - i, c[0], c[1], False), (carry, acc))
        outs.append(acc)
    o_ref[0] = jnp.where(lane < SB_HEAD_DIM, outs[0], outs[1])


def _sb_call(q, k, v, tri):
    b, s, _ = q.shape
    bq = ATTN_BLOCK
    return pl.pallas_call(
        _sb_kernel,
        grid=(b, SB_WIDTH // LANES, s // bq),
        in_specs=[pl.BlockSpec((1, bq, LANES), lambda i, p, j: (i, j, p)),
                  pl.BlockSpec((1, s, LANES), lambda i, p, j: (i, 0, p)),
                  pl.BlockSpec((1, s, LANES), lambda i, p, j: (i, 0, p)),
                  pl.BlockSpec(tri.shape, lambda i, p, j: (0, 0))],
        out_specs=pl.BlockSpec((1, bq, LANES), lambda i, p, j: (i, j, p)),
        out_shape=jax.ShapeDtypeStruct((b, s, SB_WIDTH), F32),
        compiler_params=pltpu.CompilerParams(dimension_semantics=("arbitrary",) * 3,
                                             vmem_limit_bytes=VMEM_LIMIT),
        name="stick_breaking_attn",
    )(q, k, v, tri)


def _mla_kernel(q_ref, k_ref, v_ref, o_ref):
    bq = bk = ATTN_BLOCK
    qi = pl.program_id(2)
    lane = lax.broadcasted_iota(jnp.int32, (bq, LANES), 1)
    causal = (lax.broadcasted_iota(jnp.int32, (bq, bk), 1)
              <= lax.broadcasted_iota(jnp.int32, (bq, bk), 0))

    def block(q, hh, kb, m, l, acc, masked):
        start = pl.multiple_of(kb * bk, bk)
        k = k_ref[0, pl.ds(start, bk), hh * LANES:(hh + 1) * LANES]
        v = v_ref[0, pl.ds(start, bk), :]
        sc = lax.dot_general(q, k, _NT, preferred_element_type=F32)
        if masked:
            sc = jnp.where(causal, sc, NEG_BIG)
        m_new = jnp.maximum(m, jnp.max(sc, axis=-1, keepdims=True))
        alpha = jnp.exp(m - m_new)
        p = jnp.exp(sc - m_new)
        l = alpha * l + jnp.sum(p, axis=-1, keepdims=True)
        acc = alpha * acc + _dot(p.astype(BF16), v)
        return m_new, l, acc

    outs = []
    for hh in range(HEAD_PAIR):
        q = q_ref[0, :, hh * LANES:(hh + 1) * LANES]
        init = (jnp.full((bq, 1), NEG_BIG, F32), jnp.zeros((bq, 1), F32), jnp.zeros((bq, LANES), F32))
        state = block(q, hh, qi, *init, True)
        m, l, acc = lax.fori_loop(
            0, qi, lambda i, c, q=q, hh=hh: block(q, hh, i, c[0], c[1], c[2], False), state)
        outs.append(acc / l)
    o_ref[0] = jnp.where(lane < MLA_V_DIM, outs[0], outs[1])


def _mla_call(q, k, v):
    b, s, _ = q.shape
    bq = ATTN_BLOCK
    pair = HEAD_PAIR * LANES
    return pl.pallas_call(
        _mla_kernel,
        grid=(b, MLA_HEADS // HEAD_PAIR, s // bq),
        in_specs=[pl.BlockSpec((1, bq, pair), lambda i, p, j: (i, j, p)),
                  pl.BlockSpec((1, s, pair), lambda i, p, j: (i, 0, p)),
                  pl.BlockSpec((1, s, LANES), lambda i, p, j: (i, 0, p))],
        out_specs=pl.BlockSpec((1, bq, LANES), lambda i, p, j: (i, j, p)),
        out_shape=jax.ShapeDtypeStruct((b, s, MLA_WIDTH), F32),
        compiler_params=pltpu.CompilerParams(dimension_semantics=("arbitrary",) * 3,
                                             vmem_limit_bytes=VMEM_LIMIT),
        name="mla_attn",
    )(q, k, v)


def _final_kernel(x_ref, mod_ref, ng_ref, oa_ref, ob_ref, w2_ref, wa_ref, wb_ref, wo_ref, fg_ref, out_ref,
                  *, final_norm):
    x = x_ref[0]
    hb = _adaln(x, mod_ref, ng_ref).astype(BF16)
    gate = mod_ref[0, 2:3, :]

    def branch(o_ref_, z_lo, g_lo, w_ref_):
        z = _dot(hb, w2_ref[:, z_lo:z_lo + SB_WIDTH])
        u = (o_ref_[0] * (z * _sigmoid(z))).astype(BF16)
        g = _dot(hb, w2_ref[:, g_lo:g_lo + D_MODEL])
        return _sigmoid(g) * _dot(u, w_ref_[...])

    merged = (branch(oa_ref, 0, 2 * SB_WIDTH, wa_ref)
              + branch(ob_ref, SB_WIDTH, 2 * SB_WIDTH + D_MODEL, wb_ref))
    xn = x + gate * _dot(merged.astype(BF16), wo_ref[...])
    out_ref[0] = _rms_scale(xn) * fg_ref[...] if final_norm else xn


def _final_call(x, mod3, ng, o_a, o_b, w2, wa, wb, wo, fg, final_norm):
    b, s, _ = x.shape
    tm = ROW_TILE
    row = lambda width: pl.BlockSpec((1, tm, width), lambda i, j: (i, j, 0))
    full = lambda a: pl.BlockSpec(a.shape, lambda i, j: (0,) * a.ndim)
    return pl.pallas_call(
        functools.partial(_final_kernel, final_norm=final_norm),
        grid=(b, s // tm),
        in_specs=[row(D_MODEL),
                  pl.BlockSpec((1, 3, D_MODEL), lambda i, j: (i, 0, 0)),
                  full(ng), row(SB_WIDTH), row(MLA_WIDTH),
                  full(w2), full(wa), full(wb), full(wo), full(fg)],
        out_specs=row(D_MODEL),
        out_shape=jax.ShapeDtypeStruct((b, s, D_MODEL), F32),
        compiler_params=pltpu.CompilerParams(dimension_semantics=("arbitrary", "arbitrary"),
                                             vmem_limit_bytes=VMEM_LIMIT),
        name="gated_output",
    )(x, mod3, ng, o_a, o_b, w2, wa, wb, wo, fg)


def _rotary_partner(w):
    half = w.shape[-1] // 2
    return jnp.concatenate([-w[..., half:], w[..., :half]], axis=-1)


def _pad_heads(w, heads, width):
    k = w.shape[0]
    w = w.reshape(k, heads, width)
    return jnp.pad(w, ((0, 0), (0, 0), (0, LANES - width))).reshape(k, heads * LANES)


def _layer_weights(w_in, w_uq, w_ukv):
    sb_q, sb_k, sb_v, sb_z, c_q, c_kv, k_rot, mla_z, g_a, g_b = jnp.split(
        w_in, np.cumsum([SB_WIDTH] * 4 + [MLA_Q_RANK, MLA_KV_RANK, MLA_ROPE_DIM, MLA_WIDTH, D_MODEL])
        .tolist(), axis=-1)
    place = lambda w: jnp.pad(w, ((0, 0), (MLA_NOPE_DIM, LANES - MLA_QK_DIM)))
    w1 = jnp.concatenate([sb_q * (1.0 / math.sqrt(SB_HEAD_DIM)), sb_k, sb_v, c_q, c_kv,
                          place(k_rot), place(_rotary_partner(k_rot))], axis=-1).astype(BF16)
    w2 = jnp.concatenate([sb_z, mla_z, g_a, g_b], axis=-1).astype(BF16)

    uq = w_uq.reshape(MLA_Q_RANK, MLA_HEADS, MLA_QK_DIM)
    uq_rot = jnp.concatenate([jnp.zeros_like(uq[..., :MLA_NOPE_DIM]),
                              _rotary_partner(uq[..., MLA_NOPE_DIM:])], axis=-1)
    wuq = jnp.concatenate([_pad_heads(uq.reshape(MLA_Q_RANK, -1), MLA_HEADS, MLA_QK_DIM),
                           _pad_heads(uq_rot.reshape(MLA_Q_RANK, -1), MLA_HEADS, MLA_QK_DIM)],
                          axis=-1).astype(BF16)

    ukv = w_ukv.reshape(MLA_KV_RANK, MLA_HEADS, MLA_NOPE_DIM + MLA_V_DIM)
    wk = _pad_heads(ukv[..., :MLA_NOPE_DIM].reshape(MLA_KV_RANK, -1), MLA_HEADS, MLA_NOPE_DIM)
    wv = ukv[..., MLA_NOPE_DIM:].reshape(MLA_KV_RANK, MLA_WIDTH)
    wukv = jnp.concatenate([wk, wv], axis=-1).astype(BF16)
    return w1, w2, wuq, wukv


def _rotary_freqs():
    inv_freq = ROPE_BASE ** (-jnp.arange(0, MLA_ROPE_DIM, 2, dtype=F32) / MLA_ROPE_DIM)
    pattern = jnp.concatenate([jnp.zeros((MLA_NOPE_DIM,), F32), inv_freq, inv_freq,
                               jnp.zeros((LANES - MLA_QK_DIM,), F32)])
    return pattern.reshape(1, LANES)


def kernel(x, c, positions, w_ada, b_ada, norm_gain, w_in, q_norm_gain, w_uq, kv_norm_gain, w_ukv,
           w_branch_a, w_branch_b, w_out, final_norm_gain):
    b, s, d = x.shape
    depth = w_in.shape[0]
    assert d == D_MODEL and s % ROW_TILE == 0 and s % ATTN_BLOCK == 0

    pos3 = positions.reshape(b, s, 1)
    invf = _rotary_freqs()
    idx = jnp.arange(ATTN_BLOCK)
    tri = (idx[:, None] >= idx[None, :]).astype(BF16)
    c_pad = jnp.pad(c, ((0, 8 - b % 8 if b % 8 else 0), (0, 0)))

    for l in range(depth):
        w1, w2, wuq, wukv = _layer_weights(w_in[l], w_uq[l], w_ukv[l])
        mod = _mod_call(c_pad, w_ada[l], b_ada[l].reshape(1, -1))[:b]
        mod3 = mod.reshape(b, 3, D_MODEL)
        ng = norm_gain[l].reshape(1, D_MODEL)

        q_sb, k_sb, v_sb, q_f, k_f, v_m = _proj_call(
            x, mod3, ng, pos3, invf, w1, q_norm_gain[l].reshape(1, -1), wuq,
            kv_norm_gain[l].reshape(1, -1), wukv)
        o_a = _sb_call(q_sb, k_sb, v_sb, tri)
        o_b = _mla_call(q_f, k_f, v_m)
        x = _final_call(x, mod3, ng, o_a, o_b, w2, w_branch_a[l].astype(BF16),
                        w_branch_b[l].astype(BF16), w_out[l].astype(BF16),
                        final_norm_gain.reshape(1, D_MODEL), final_norm=(l == depth - 1))
    return x
```

```python
import functools
import math

import numpy as np
import jax
import jax.numpy as jnp
from jax import lax
from jax.experimental import pallas as pl
from jax.experimental.pallas import tpu as pltpu

F32 = jnp.float32
BF16 = jnp.bfloat16

D_MODEL = 1024
SB_HEADS = 8
SB_HEAD_DIM = 64
SB_WIDTH = SB_HEADS * SB_HEAD_DIM
MLA_HEADS = 8
MLA_NOPE_DIM = 64
MLA_ROPE_DIM = 32
MLA_V_DIM = 64
MLA_Q_RANK = 384
MLA_KV_RANK = 256
MLA_WIDTH = MLA_HEADS * MLA_V_DIM
MLA_QK_DIM = MLA_NOPE_DIM + MLA_ROPE_DIM
ROPE_BASE = 10000.0
EPS = 1e-6

LANES = 128
HEAD_PAIR = 2
MLA_PAD_WIDTH = MLA_HEADS * LANES

ROW_TILE = 512
ATTN_BLOCK = 512
CUMSUM_BLOCK = 256
VMEM_LIMIT = 56 * 1024 * 1024

NEG_BIG = -1e30

_C_SBQ = 0
_C_SBK = _C_SBQ + SB_WIDTH
_C_SBV = _C_SBK + SB_WIDTH
_C_CQ = _C_SBV + SB_WIDTH
_C_CKV = _C_CQ + MLA_Q_RANK
_C_KP = _C_CKV + MLA_KV_RANK
_C_KR = _C_KP + LANES
_C_END = _C_KR + LANES

_NT = (((1,), (1,)), ((), ()))


def _dot(a, b):
    return jnp.dot(a, b, preferred_element_type=F32)


def _split_bf16(v):
    hi = v.astype(BF16)
    lo = (v - hi.astype(F32)).astype(BF16)
    return hi, lo


def _sigmoid(v):
    return 1.0 / (1.0 + jnp.exp(-v))


def _rms_scale(v):
    return v * lax.rsqrt(jnp.mean(v * v, axis=-1, keepdims=True) + EPS)


def _adaln(x, mod_ref, ng_ref):
    shift = mod_ref[0, 0:1, :]
    scale = mod_ref[0, 1:2, :]
    return _rms_scale(x) * ng_ref[...] * (1.0 + scale) + shift


def _mod_kernel(c_ref, w_ref, b_ref, o_ref):
    c_hi, c_lo = _split_bf16(c_ref[...])
    w_hi, w_lo = _split_bf16(w_ref[...])
    o_ref[...] = _dot(c_hi, w_hi) + _dot(c_hi, w_lo) + _dot(c_lo, w_hi) + b_ref[...]


def _mod_call(c_pad, w_ada, b_ada):
    rows = c_pad.shape[0]
    n = w_ada.shape[1]
    tn = D_MODEL
    return pl.pallas_call(
        _mod_kernel,
        grid=(n // tn,),
        in_specs=[pl.BlockSpec((rows, D_MODEL), lambda j: (0, 0)),
                  pl.BlockSpec((D_MODEL, tn), lambda j: (0, j)),
                  pl.BlockSpec((1, tn), lambda j: (0, j))],
        out_specs=pl.BlockSpec((rows, tn), lambda j: (0, j)),
        out_shape=jax.ShapeDtypeStruct((rows, n), F32),
        compiler_params=pltpu.CompilerParams(dimension_semantics=("arbitrary",),
                                             vmem_limit_bytes=VMEM_LIMIT),
        name="adaln_mod",
    )(c_pad, w_ada, b_ada)


def _proj_kernel(x_ref, mod_ref, ng_ref, pos_ref, invf_ref, w1_ref, qg_ref, wuq_ref, kvg_ref, wukv_ref,
                 qsb_ref, ksb_ref, vsb_ref, qf_ref, kf_ref, vm_ref):
    hb = _adaln(x_ref[0], mod_ref, ng_ref).astype(BF16)

    qsb_ref[0] = _dot(hb, w1_ref[:, _C_SBQ:_C_SBK]).astype(BF16)
    ksb_ref[0] = _dot(hb, w1_ref[:, _C_SBK:_C_SBV]).astype(BF16)
    vsb_ref[0] = _dot(hb, w1_ref[:, _C_SBV:_C_CQ]).astype(BF16)

    ang = pos_ref[0].astype(F32) * invf_ref[...]
    cosv = jnp.cos(ang)
    sinv = jnp.sin(ang)

    cqn = (_rms_scale(_dot(hb, w1_ref[:, _C_CQ:_C_CKV])) * qg_ref[...]).astype(BF16)
    q_scale = 1.0 / math.sqrt(MLA_QK_DIM)
    cos_q = cosv * q_scale
    sin_q = sinv * q_scale
    for hh in range(MLA_HEADS):
        lo = hh * LANES
        q_pad = _dot(cqn, wuq_ref[:, lo:lo + LANES])
        q_rot = _dot(cqn, wuq_ref[:, MLA_PAD_WIDTH + lo:MLA_PAD_WIDTH + lo + LANES])
        qf_ref[0, :, lo:lo + LANES] = (q_pad * cos_q + q_rot * sin_q).astype(BF16)

    ckvn = (_rms_scale(_dot(hb, w1_ref[:, _C_CKV:_C_KP])) * kvg_ref[...]).astype(BF16)
    k_pe = _dot(hb, w1_ref[:, _C_KP:_C_KR]) * cosv + _dot(hb, w1_ref[:, _C_KR:_C_END]) * sinv
    for hh in range(MLA_HEADS):
        lo = hh * LANES
        kf_ref[0, :, lo:lo + LANES] = (_dot(ckvn, wukv_ref[:, lo:lo + LANES]) + k_pe).astype(BF16)
    vm_ref[0] = _dot(ckvn, wukv_ref[:, MLA_PAD_WIDTH:MLA_PAD_WIDTH + MLA_WIDTH]).astype(BF16)


def _proj_call(x, mod3, ng, pos3, invf, w1, qg, wuq, kvg, wukv):
    b, s, _ = x.shape
    tm = ROW_TILE
    row = lambda width: pl.BlockSpec((1, tm, width), lambda i, j: (i, j, 0))
    full = lambda a: pl.BlockSpec(a.shape, lambda i, j: (0,) * a.ndim)
    out = lambda width: jax.ShapeDtypeStruct((b, s, width), BF16)
    return pl.pallas_call(
        _proj_kernel,
        grid=(b, s // tm),
        in_specs=[row(D_MODEL),
                  pl.BlockSpec((1, 3, D_MODEL), lambda i, j: (i, 0, 0)),
                  full(ng), row(1), full(invf), full(w1), full(qg), full(wuq), full(kvg), full(wukv)],
        out_specs=[row(SB_WIDTH), row(SB_WIDTH), row(SB_WIDTH),
                   row(MLA_PAD_WIDTH), row(MLA_PAD_WIDTH), row(MLA_WIDTH)],
        out_shape=[out(SB_WIDTH), out(SB_WIDTH), out(SB_WIDTH),
                   out(MLA_PAD_WIDTH), out(MLA_PAD_WIDTH), out(MLA_WIDTH)],
        compiler_params=pltpu.CompilerParams(dimension_semantics=("arbitrary", "arbitrary"),
                                             vmem_limit_bytes=VMEM_LIMIT),
        name="attn_operand_proj",
    )(x, mod3, ng, pos3, invf, w1, qg, wuq, kvg, wukv)


def _sb_kernel(q_ref, k_ref, v_ref, tri_ref, o_ref):
    bq = bk = ATTN_BLOCK
    qi = pl.program_id(2)
    q = q_ref[0]
    lane = lax.broadcasted_iota(jnp.int32, (bq, LANES), 1)
    strict = (lax.broadcasted_iota(jnp.int32, (bq, bk), 1)
              < lax.broadcasted_iota(jnp.int32, (bq, bk), 0))
    tri = tri_ref[...]
    qs = [jnp.where((lane >= hh * SB_HEAD_DIM) & (lane < (hh + 1) * SB_HEAD_DIM), q, jnp.zeros_like(q))
          for hh in range(HEAD_PAIR)]

    def step(kb, state, masked):
        start = pl.multiple_of(kb * bk, bk)
        k = k_ref[0, pl.ds(start, bk), :]
        v = v_ref[0, pl.ds(start, bk), :]
        new_state = []
        for qh, (carry, acc) in zip(qs, state):
            z = lax.dot_general(qh, k, _NT, preferred_element_type=F32)
            sp = jnp.maximum(z, 0.0) + jnp.log(1.0 + jnp.exp(-jnp.abs(z)))
            if masked:
                sp = jnp.where(strict, sp, 0.0)
            sp = sp.astype(BF16)
            parts = []
            tail = carry
            for c in reversed(range(bk // CUMSUM_BLOCK)):
                part = _dot(sp[:, c * CUMSUM_BLOCK:(c + 1) * CUMSUM_BLOCK], tri) + tail
                parts.append(part)
                tail = part[:, 0:1]
            incl = jnp.concatenate(parts[::-1], axis=1)
            w = jnp.exp(z - incl)
            if masked:
                w = jnp.where(strict, w, 0.0)
            new_state.append((tail, acc + _dot(w.astype(BF16), v)))
        return tuple(new_state)

    init = tuple((jnp.zeros((bq, 1), F32), jnp.zeros((bq, LANES), F32)) for _ in range(HEAD_PAIR))
    state = step(qi, init, True)
    state = lax.fori_loop(0, qi, lambda i, st: step(qi - 1 - i, st, False), state)
    o_ref[0] = jnp.where(lane < SB_HEAD_DIM, state[0][1], state[1][1])


def _sb_call(q, k, v, tri):
    b, s, _ = q.shape
    bq = ATTN_BLOCK
    return pl.pallas_call(
        _sb_kernel,
        grid=(b, SB_WIDTH // LANES, s // bq),
        in_specs=[pl.BlockSpec((1, bq, LANES), lambda i, p, j: (i, j, p)),
                  pl.BlockSpec((1, s, LANES), lambda i, p, j: (i, 0, p)),
                  pl.BlockSpec((1, s, LANES), lambda i, p, j: (i, 0, p)),
                  pl.BlockSpec(tri.shape, lambda i, p, j: (0, 0))],
        out_specs=pl.BlockSpec((1, bq, LANES), lambda i, p, j: (i, j, p)),
        out_shape=jax.ShapeDtypeStruct((b, s, SB_WIDTH), F32),
        compiler_params=pltpu.CompilerParams(dimension_semantics=("arbitrary",) * 3,
                                             vmem_limit_bytes=VMEM_LIMIT),
        name="stick_breaking_attn",
    )(q, k, v, tri)


def _mla_kernel(q_ref, k_ref, v_ref, o_ref):
    bq = bk = ATTN_BLOCK
    qi = pl.program_id(2)
    lane = lax.broadcasted_iota(jnp.int32, (bq, LANES), 1)
    causal = (lax.broadcasted_iota(jnp.int32, (bq, bk), 1)
              <= lax.broadcasted_iota(jnp.int32, (bq, bk), 0))

    qs = [q_ref[0, :, hh * LANES:(hh + 1) * LANES] for hh in range(HEAD_PAIR)]

    def step(kb, state, masked):
        start = pl.multiple_of(kb * bk, bk)
        v = v_ref[0, pl.ds(start, bk), :]
        new_state = []
        for hh, (m, l, acc) in enumerate(state):
            k = k_ref[0, pl.ds(start, bk), hh * LANES:(hh + 1) * LANES]
            sc = lax.dot_general(qs[hh], k, _NT, preferred_element_type=F32)
            if masked:
                sc = jnp.where(causal, sc, NEG_BIG)
            m_new = jnp.maximum(m, jnp.max(sc, axis=-1, keepdims=True))
            alpha = jnp.exp(m - m_new)
            p = jnp.exp(sc - m_new)
            l = alpha * l + jnp.sum(p, axis=-1, keepdims=True)
            new_state.append((m_new, l, alpha * acc + _dot(p.astype(BF16), v)))
        return tuple(new_state)

    init = tuple((jnp.full((bq, 1), NEG_BIG, F32), jnp.zeros((bq, 1), F32), jnp.zeros((bq, LANES), F32))
                 for _ in range(HEAD_PAIR))
    state = step(qi, init, True)
    state = lax.fori_loop(0, qi, lambda i, st: step(i, st, False), state)
    outs = [acc / l for (_, l, acc) in state]
    o_ref[0] = jnp.where(lane < MLA_V_DIM, outs[0], outs[1])


def _mla_call(q, k, v):
    b, s, _ = q.shape
    bq = ATTN_BLOCK
    pair = HEAD_PAIR * LANES
    return pl.pallas_call(
        _mla_kernel,
        grid=(b, MLA_HEADS // HEAD_PAIR, s // bq),
        in_specs=[pl.BlockSpec((1, bq, pair), lambda i, p, j: (i, j, p)),
                  pl.BlockSpec((1, s, pair), lambda i, p, j: (i, 0, p)),
                  pl.BlockSpec((1, s, LANES), lambda i, p, j: (i, 0, p))],
        out_specs=pl.BlockSpec((1, bq, LANES), lambda i, p, j: (i, j, p)),
        out_shape=jax.ShapeDtypeStruct((b, s, MLA_WIDTH), F32),
        compiler_params=pltpu.CompilerParams(dimension_semantics=("arbitrary",) * 3,
                                             vmem_limit_bytes=VMEM_LIMIT),
        name="mla_attn",
    )(q, k, v)


def _final_kernel(x_ref, mod_ref, ng_ref, oa_ref, ob_ref, w2_ref, wa_ref, wb_ref, wo_ref, fg_ref, out_ref,
                  *, final_norm):
    x = x_ref[0]
    hb = _adaln(x, mod_ref, ng_ref).astype(BF16)
    gate = mod_ref[0, 2:3, :]

    def branch(o_ref_, z_lo, g_lo, w_ref_):
        z = _dot(hb, w2_ref[:, z_lo:z_lo + SB_WIDTH])
        u = (o_ref_[0] * (z * _sigmoid(z))).astype(BF16)
        g = _dot(hb, w2_ref[:, g_lo:g_lo + D_MODEL])
        return _sigmoid(g) * _dot(u, w_ref_[...])

    merged = (branch(oa_ref, 0, 2 * SB_WIDTH, wa_ref)
              + branch(ob_ref, SB_WIDTH, 2 * SB_WIDTH + D_MODEL, wb_ref))
    xn = x + gate * _dot(merged.astype(BF16), wo_ref[...])
    out_ref[0] = _rms_scale(xn) * fg_ref[...] if final_norm else xn


def _final_call(x, mod3, ng, o_a, o_b, w2, wa, wb, wo, fg, final_norm):
    b, s, _ = x.shape
    tm = ROW_TILE
    row = lambda width: pl.BlockSpec((1, tm, width), lambda i, j: (i, j, 0))
    full = lambda a: pl.BlockSpec(a.shape, lambda i, j: (0,) * a.ndim)
    return pl.pallas_call(
        functools.partial(_final_kernel, final_norm=final_norm),
        grid=(b, s // tm),
        in_specs=[row(D_MODEL),
                  pl.BlockSpec((1, 3, D_MODEL), lambda i, j: (i, 0, 0)),
                  full(ng), row(SB_WIDTH), row(MLA_WIDTH),
                  full(w2), full(wa), full(wb), full(wo), full(fg)],
        out_specs=row(D_MODEL),
        out_shape=jax.ShapeDtypeStruct((b, s, D_MODEL), F32),
        compiler_params=pltpu.CompilerParams(dimension_semantics=("arbitrary", "arbitrary"),
                                             vmem_limit_bytes=VMEM_LIMIT),
        name="gated_output",
    )(x, mod3, ng, o_a, o_b, w2, wa, wb, wo, fg)


def _rotary_partner(w):
    half = w.shape[-1] // 2
    return jnp.concatenate([-w[..., half:], w[..., :half]], axis=-1)


def _pad_heads(w, heads, width):
    k = w.shape[0]
    w = w.reshape(k, heads, width)
    return jnp.pad(w, ((0, 0), (0, 0), (0, LANES - width))).reshape(k, heads * LANES)


def _layer_weights(w_in, w_uq, w_ukv):
    sb_q, sb_k, sb_v, sb_z, c_q, c_kv, k_rot, mla_z, g_a, g_b = jnp.split(
        w_in, np.cumsum([SB_WIDTH] * 4 + [MLA_Q_RANK, MLA_KV_RANK, MLA_ROPE_DIM, MLA_WIDTH, D_MODEL])
        .tolist(), axis=-1)
    place = lambda w: jnp.pad(w, ((0, 0), (MLA_NOPE_DIM, LANES - MLA_QK_DIM)))
    w1 = jnp.concatenate([sb_q * (1.0 / math.sqrt(SB_HEAD_DIM)), sb_k, sb_v, c_q, c_kv,
                          place(k_rot), place(_rotary_partner(k_rot))], axis=-1).astype(BF16)
    w2 = jnp.concatenate([sb_z, mla_z, g_a, g_b], axis=-1).astype(BF16)

    uq = w_uq.reshape(MLA_Q_RANK, MLA_HEADS, MLA_QK_DIM)
    uq_rot = jnp.concatenate([jnp.zeros_like(uq[..., :MLA_NOPE_DIM]),
                              _rotary_partner(uq[..., MLA_NOPE_DIM:])], axis=-1)
    wuq = jnp.concatenate([_pad_heads(uq.reshape(MLA_Q_RANK, -1), MLA_HEADS, MLA_QK_DIM),
                           _pad_heads(uq_rot.reshape(MLA_Q_RANK, -1), MLA_HEADS, MLA_QK_DIM)],
                          axis=-1).astype(BF16)

    ukv = w_ukv.reshape(MLA_KV_RANK, MLA_HEADS, MLA_NOPE_DIM + MLA_V_DIM)
    wk = _pad_heads(ukv[..., :MLA_NOPE_DIM].reshape(MLA_KV_RANK, -1), MLA_HEADS, MLA_NOPE_DIM)
    wv = ukv[..., MLA_NOPE_DIM:].reshape(MLA_KV_RANK, MLA_WIDTH)
    wukv = jnp.concatenate([wk, wv], axis=-1).astype(BF16)
    return w1, w2, wuq, wukv


def _rotary_freqs():
    inv_freq = ROPE_BASE ** (-jnp.arange(0, MLA_ROPE_DIM, 2, dtype=F32) / MLA_ROPE_DIM)
    pattern = jnp.concatenate([jnp.zeros((MLA_NOPE_DIM,), F32), inv_freq, inv_freq,
                               jnp.zeros((LANES - MLA_QK_DIM,), F32)])
    return pattern.reshape(1, LANES)


def kernel(x, c, positions, w_ada, b_ada, norm_gain, w_in, q_norm_gain, w_uq, kv_norm_gain, w_ukv,
           w_branch_a, w_branch_b, w_out, final_norm_gain):
    b, s, d = x.shape
    depth = w_in.shape[0]
    assert d == D_MODEL and s % ROW_TILE == 0 and s % ATTN_BLOCK == 0

    pos3 = positions.reshape(b, s, 1)
    invf = _rotary_freqs()
    idx = jnp.arange(CUMSUM_BLOCK)
    tri = (idx[:, None] >= idx[None, :]).astype(BF16)
    c_pad = jnp.pad(c, ((0, 8 - b % 8 if b % 8 else 0), (0, 0)))

    for l in range(depth):
        w1, w2, wuq, wukv = _layer_weights(w_in[l], w_uq[l], w_ukv[l])
        mod = _mod_call(c_pad, w_ada[l], b_ada[l].reshape(1, -1))[:b]
        mod3 = mod.reshape(b, 3, D_MODEL)
        ng = norm_gain[l].reshape(1, D_MODEL)

        q_sb, k_sb, v_sb, q_f, k_f, v_m = _proj_call(
            x, mod3, ng, pos3, invf, w1, q_norm_gain[l].reshape(1, -1), wuq,
            kv_norm_gain[l].reshape(1, -1), wukv)
        o_a = _sb_call(q_sb, k_sb, v_sb, tri)
        o_b = _mla_call(q_f, k_f, v_m)
        x = _final_call(x, mod3, ng, o_a, o_b, w2, w_branch_a[l].astype(BF16),
                        w_branch_b[l].astype(BF16), w_out[l].astype(BF16),
                        final_norm_gain.reshape(1, D_MODEL), final_norm=(l == depth - 1))
    return x
```
